```python
import math
import jax, jax.numpy as jnp
from jax import lax
import numpy as np

D_MODEL = 2048
BATCH = 16
SEQ = 256
DEPTH = 2
DEC_BATCH = 4
DEC_SEQ = 2048
PAST_LEN = 512

GRID_W = 64
HEAD_DIM = 64
D_ATTN = D_MODEL // 2
NA_HEADS = D_ATTN // HEAD_DIM
NA_KH = 8
NA_KW = 16
NA_QB = 16
NA_KB = NA_QB + NA_KW
D_CONV = D_MODEL // 4
CONV_K = 31
D_SSM = D_MODEL // 4
SSM_P = 64
SSM_HEADS = D_SSM // SSM_P
SSM_GROUPS = 2
SSM_N = 128
SSM_CONV = 5
SSM_CHUNK = 128
D_XBC = D_SSM + 2 * SSM_GROUPS * SSM_N
D_IN = 3 * D_ATTN + 2 * D_CONV + D_SSM + D_XBC + 2 * SSM_HEADS
D_FF = 256 * ((8 * D_MODEL // 3 + 255) // 256)
FFN_CONV = 3
ALPHA = (2 * DEPTH) ** 0.25
BETA = (8 * DEPTH) ** -0.25
LN_EPS = 1e-5
NEG_INF = -1e30

kernel_name = "hybrid_na_conformer_ssd_dit_step"


def layer_norm(x, g, b):
    xf = x.astype(jnp.float32)
    mu = jnp.mean(xf, axis=-1, keepdims=True)
    var = jnp.mean(jnp.square(xf - mu), axis=-1, keepdims=True)
    y = (xf - mu) * lax.rsqrt(var + LN_EPS)
    return (y * g.astype(jnp.float32) + b.astype(jnp.float32)).astype(x.dtype)


def rms_norm(x, g):
    xf = x.astype(jnp.float32)
    y = xf * lax.rsqrt(jnp.mean(jnp.square(xf), axis=-1, keepdims=True) + LN_EPS)
    return y * g.astype(jnp.float32)


def dwconv(x, w, b):
    k = w.shape[0]
    y = lax.conv_general_dilated(x, w[:, None, :], (1,), [(k // 2, k // 2)],
                                 dimension_numbers=("NWC", "WIO", "NWC"),
                                 feature_group_count=x.shape[-1])
    return y + b


def modulation(cond, w, b):
    m = jax.nn.silu(cond) @ w + b
    return jnp.split(m[:, None, :], 6, axis=-1)


def context_attention(q, k, v):
    s = jnp.einsum("bqhd,bkhd->bhqk", q, k, preferred_element_type=jnp.float32) * HEAD_DIM ** -0.5
    p = jax.nn.softmax(s, axis=-1).astype(v.dtype)
    return jnp.einsum("bhqk,bkhd->bqhd", p, v)


def neighborhood_attention(q, k, v, k_ctx, v_ctx, rpb):
    b, t, h, d = q.shape
    rows = t // GRID_W
    kh = min(NA_KH, rows)
    ncb = GRID_W // NA_QB
    r = jnp.arange(rows)
    row_idx = jnp.clip(r - kh // 2, 0, rows - kh)[:, None] + jnp.arange(kh)
    cols = jnp.arange(GRID_W).reshape(ncb, NA_QB)
    col_start = jnp.clip(cols - NA_KW // 2, 0, GRID_W - NA_KW)
    col_idx = (jnp.clip(jnp.arange(ncb) * NA_QB - NA_KW // 2, 0, GRID_W - NA_KB)[:, None]
               + jnp.arange(NA_KB))
    in_win = ((col_idx[:, None, :] >= col_start[..., None])
              & (col_idx[:, None, :] < col_start[..., None] + NA_KW))
    rel_r = row_idx - r[:, None] + NA_KH - 1
    rel_c = jnp.clip(col_idx[:, None, :] - cols[..., None] + NA_KW - 1, 0, 2 * NA_KW - 2)
    bias = rpb.astype(jnp.float32)[:, rel_r[:, None, None, :, None], rel_c[None, :, :, None, :]]
    bias = jnp.where(in_win[None, None, :, :, None, :], bias, NEG_INF)
    bias = bias.reshape(h, rows, ncb, NA_QB, kh * NA_KB)
    n_lat = kh * NA_KB
    ridx = row_idx[:, None, :, None]
    cidx = col_idx[None, :, None, :]
    k_blk = k.reshape(b, rows, GRID_W, h, d)[:, ridx, cidx].reshape(b, rows, ncb, n_lat, h, d)
    v_blk = v.reshape(b, rows, GRID_W, h, d)[:, ridx, cidx].reshape(b, rows, ncb, n_lat, h, d)
    qb = q.reshape(b, rows, ncb, NA_QB, h, d)
    scale = HEAD_DIM ** -0.5
    s_lat = jnp.einsum("brjqhd,brjkhd->bhrjqk", qb, k_blk, preferred_element_type=jnp.float32) * scale + bias
    s_ctx = jnp.einsum("brjqhd,bmhd->bhrjqm", qb, k_ctx, preferred_element_type=jnp.float32) * scale
    p = jax.nn.softmax(jnp.concatenate([s_lat, s_ctx], axis=-1), axis=-1).astype(v.dtype)
    o = (jnp.einsum("bhrjqk,brjkhd->brjqhd", p[..., :n_lat], v_blk)
         + jnp.einsum("bhrjqm,bmhd->brjqhd", p[..., n_lat:], v_ctx))
    return o.reshape(b, t, h * d)


def conformer_conv(u, g, w, bconv, ln_g, ln_b):
    hcur = dwconv(u * jax.nn.sigmoid(g), w, bconv)
    return jax.nn.silu(layer_norm(hcur, ln_g, ln_b))


def ssd_scan(x, dt, a, bm, cm, h0):
    b, t, h, p = x.shape
    n = bm.shape[-1]
    q = SSM_CHUNK
    nc = t // q
    x = x.reshape(b, nc, q, h, p)
    dt = dt.reshape(b, nc, q, h)
    bm = bm.reshape(b, nc, q, h, n)
    cm = cm.reshape(b, nc, q, h, n)
    cs = jnp.cumsum(dt * a, axis=2)
    tril = jnp.tril(jnp.ones((q, q), dtype=bool))[None, None, :, :, None]
    seg = jnp.exp(jnp.where(tril, cs[:, :, :, None, :] - cs[:, :, None, :, :], NEG_INF))
    xdt = x * dt[..., None]
    scores = jnp.einsum("bcihn,bcjhn->bcijh", cm, bm) * seg
    y_diag = jnp.einsum("bcijh,bcjhp->bcihp", scores, xdt)
    decay_end = jnp.exp(cs[:, :, -1:, :] - cs)
    states = jnp.einsum("bcjhn,bcjhp->bchpn", bm * decay_end[..., None], xdt)
    chunk_decay = jnp.exp(cs[:, :, -1, :])

    def step(hc, inp):
        st, dec = inp
        return hc * dec[:, :, None, None] + st, hc

    h_last, h_start = lax.scan(step, h0, (jnp.moveaxis(states, 1, 0), jnp.moveaxis(chunk_decay, 1, 0)))
    h_start = jnp.moveaxis(h_start, 0, 1)
    y_off = jnp.einsum("bcihn,bchpn->bcihp", cm * jnp.exp(cs)[..., None], h_start)
    return (y_diag + y_off).reshape(b, t, h, p), h_last


def ssd_mixer(z, xbc, dt_raw, conv_w, conv_b, a_log, dt_bias, d_skip, norm_g, h0):
    b, t, _ = z.shape
    f32 = jnp.float32
    xbc = jax.nn.silu(dwconv(xbc, conv_w, conv_b)).astype(f32)
    xs, bm, cm = jnp.split(xbc, [D_SSM, D_SSM + SSM_GROUPS * SSM_N], axis=-1)
    xs = xs.reshape(b, t, SSM_HEADS, SSM_P)
    rep = SSM_HEADS // SSM_GROUPS
    bm = jnp.repeat(bm.reshape(b, t, SSM_GROUPS, SSM_N), rep, axis=2)
    cm = jnp.repeat(cm.reshape(b, t, SSM_GROUPS, SSM_N), rep, axis=2)
    a = -jnp.exp(a_log.astype(f32))
    dt = jax.nn.softplus(dt_raw.astype(f32).reshape(b, t, 2, SSM_HEADS) + dt_bias.astype(f32))
    h0 = h0.astype(f32)
    y_f, h_f = ssd_scan(xs, dt[:, :, 0], a[0], bm, cm, h0[:, 0])
    flip = lambda u: jnp.flip(u, axis=1)
    y_b, h_b = ssd_scan(flip(xs), flip(dt[:, :, 1]), a[1], flip(bm), flip(cm), h0[:, 1])
    y = y_f + flip(y_b) + d_skip.astype(f32)[:, None] * xs
    y = y.reshape(b, t, D_SSM) * jax.nn.silu(z.astype(f32))
    return rms_norm(y, norm_g).astype(z.dtype), jnp.stack([h_f, h_b], axis=1)


def conv_ffn(hcur, w_up, cw, cb, w_down):
    u = dwconv(hcur @ w_up, cw, cb)
    a, g = jnp.split(u, 2, axis=-1)
    return (a * jax.nn.silu(g)) @ w_down


def trunk_layer(x, cond, lp, ctx=None):
    b, t, _ = x.shape
    sh1, sc1, g1, sh2, sc2, g2 = modulation(cond, lp["w_mod"], lp["b_mod"])
    hcur = x * (1 + sc1) + sh1
    cuts, acc = [], 0
    for s in (D_ATTN, D_ATTN, D_ATTN, D_CONV, D_CONV, D_SSM, D_XBC):
        acc += s
        cuts.append(acc)
    q, k, v, u, gt, z, xbc, dt = jnp.split(hcur @ lp["w_in"], cuts, axis=-1)
    q = q.reshape(b, t, NA_HEADS, HEAD_DIM)
    k = k.reshape(b, t, NA_HEADS, HEAD_DIM)
    v = v.reshape(b, t, NA_HEADS, HEAD_DIM)
    if ctx is None:
        o_a = context_attention(q, k, v).reshape(b, t, D_ATTN)
        h0 = jnp.zeros((b, 2, SSM_HEADS, SSM_P, SSM_N), jnp.float32)
    else:
        k_ctx, v_ctx, h0 = ctx
        o_a = neighborhood_attention(q, k, v, k_ctx, v_ctx, lp["rpb"])
    o_b = conformer_conv(u, gt, lp["conv_w"], lp["conv_b"], lp["conv_ln_g"], lp["conv_ln_b"])
    o_c, h_state = ssd_mixer(z, xbc, dt, lp["ssm_conv_w"], lp["ssm_conv_b"], lp["ssm_a_log"],
                             lp["ssm_dt_bias"], lp["ssm_d"], lp["ssm_norm_g"], h0)
    mix = jnp.concatenate([o_a, o_b, o_c], axis=-1) @ lp["w_out"]
    x = layer_norm(ALPHA * x + g1 * mix, lp["ln1_g"], lp["ln1_b"])
    hcur = x * (1 + sc2) + sh2
    ff = conv_ffn(hcur, lp["w_up"], lp["ffn_conv_w"], lp["ffn_conv_b"], lp["w_down"])
    x = layer_norm(ALPHA * x + g2 * ff, lp["ln2_g"], lp["ln2_b"])
    return x, k, v, h_state


def setup_inputs(seed: int = 0) -> dict:
    key = jax.random.key(seed)
    ks = jax.random.split(key, 32)
    f32 = jnp.float32
    L = DEPTH
    nrm = lambda kk, shape, s: jax.random.normal(kk, shape, f32) * s
    dt0 = jnp.exp(jax.random.uniform(ks[20], (L, 2, SSM_HEADS), f32, math.log(1e-3), math.log(1e-1)))
    return {
        "x_prompt": nrm(ks[0], (BATCH, SEQ, D_MODEL), 1.0),
        "x_sample": nrm(ks[1], (DEC_BATCH, DEC_SEQ, D_MODEL), 1.0),
        "cache_k": nrm(ks[2], (DEC_BATCH, DEPTH, PAST_LEN, NA_HEADS, HEAD_DIM), 1.0),
        "cache_v": nrm(ks[3], (DEC_BATCH, DEPTH, PAST_LEN, NA_HEADS, HEAD_DIM), 1.0),
        "state_ssm": nrm(ks[4], (DEC_BATCH, DEPTH, 2, SSM_HEADS, SSM_P, SSM_N), 0.5),
        "c": nrm(ks[5], (DEC_BATCH, D_MODEL), 1.0),
        "c_ctx": nrm(ks[6], (D_MODEL,), 1.0),
        "w_mod": nrm(ks[7], (L, D_MODEL, 6 * D_MODEL), 0.5 * D_MODEL ** -0.5),
        "b_mod": nrm(ks[8], (L, 6 * D_MODEL), 0.02),
        "w_in": nrm(ks[9], (L, D_MODEL, D_IN), D_MODEL ** -0.5),
        "rpb": nrm(ks[10], (L, NA_HEADS, 2 * NA_KH - 1, 2 * NA_KW - 1), 0.1),
        "conv_w": nrm(ks[11], (L, CONV_K, D_CONV), CONV_K ** -0.5),
        "conv_b": nrm(ks[12], (L, D_CONV), 0.01),
        "conv_ln_g": 1.0 + nrm(ks[13], (L, D_CONV), 0.02),
        "conv_ln_b": nrm(ks[14], (L, D_CONV), 0.02),
        "ssm_conv_w": nrm(ks[15], (L, SSM_CONV, D_XBC), SSM_CONV ** -0.5),
        "ssm_conv_b": nrm(ks[16], (L, D_XBC), 0.01),
        "ssm_a_log": jnp.log(jax.random.uniform(ks[17], (L, 2, SSM_HEADS), f32, 1.0, 16.0)),
        "ssm_dt_bias": dt0 + jnp.log(-jnp.expm1(-dt0)),
        "ssm_d": 1.0 + nrm(ks[18], (L, SSM_HEADS), 0.1),
        "ssm_norm_g": 1.0 + nrm(ks[19], (L, D_SSM), 0.02),
        "w_out": nrm(ks[21], (L, D_MODEL, D_MODEL), BETA * D_MODEL ** -0.5),
        "ln1_g": 1.0 + nrm(ks[22], (L, D_MODEL), 0.02),
        "ln1_b": nrm(ks[23], (L, D_MODEL), 0.02),
        "w_up": nrm(ks[24], (L, D_MODEL, 2 * D_FF), D_MODEL ** -0.5),
        "ffn_conv_w": nrm(ks[25], (L, FFN_CONV, 2 * D_FF), FFN_CONV ** -0.5),
        "ffn_conv_b": nrm(ks[26], (L, 2 * D_FF), 0.01),
        "w_down": nrm(ks[27], (L, D_FF, D_MODEL), BETA * D_FF ** -0.5),
        "ln2_g": 1.0 + nrm(ks[28], (L, D_MODEL), 0.02),
        "ln2_b": nrm(ks[29], (L, D_MODEL), 0.02),
    }


def reference(x_prompt, x_sample, cache_k, cache_v, state_ssm, c, c_ctx, w_mod, b_mod, w_in, rpb,
              conv_w, conv_b, conv_ln_g, conv_ln_b, ssm_conv_w, ssm_conv_b, ssm_a_log, ssm_dt_bias,
              ssm_d, ssm_norm_g, w_out, ln1_g, ln1_b, w_up, ffn_conv_w, ffn_conv_b, w_down, ln2_g, ln2_b):
    y_prompt = x_prompt
    y_sample = x_sample
    ks, vs, hs = [], [], []
    for l in range(DEPTH):
        lp = dict(w_mod=w_mod[l], b_mod=b_mod[l], w_in=w_in[l], rpb=rpb[l], conv_w=conv_w[l],
                  conv_b=conv_b[l], conv_ln_g=conv_ln_g[l], conv_ln_b=conv_ln_b[l],
                  ssm_conv_w=ssm_conv_w[l], ssm_conv_b=ssm_conv_b[l], ssm_a_log=ssm_a_log[l],
                  ssm_dt_bias=ssm_dt_bias[l], ssm_d=ssm_d[l], ssm_norm_g=ssm_norm_g[l],
                  w_out=w_out[l], ln1_g=ln1_g[l], ln1_b=ln1_b[l], w_up=w_up[l],
                  ffn_conv_w=ffn_conv_w[l], ffn_conv_b=ffn_conv_b[l], w_down=w_down[l],
                  ln2_g=ln2_g[l], ln2_b=ln2_b[l])
        y_prompt, k_l, v_l, h_l = trunk_layer(y_prompt, c_ctx[None, :], lp)
        ks.append(k_l)
        vs.append(v_l)
        hs.append(h_l)
        y_sample, _, _, _ = trunk_layer(y_sample, c, lp, (cache_k[:, l], cache_v[:, l], state_ssm[:, l]))
    new_cache_k = jnp.stack(ks, axis=1)
    new_cache_v = jnp.stack(vs, axis=1)
    new_state_ssm = jnp.stack(hs, axis=1)
    return (y_prompt, y_sample, new_cache_k, new_cache_v, new_state_ssm)
```

```python
import functools

import jax
import jax.numpy as jnp
import numpy as np
from jax import lax
from jax.experimental import pallas as pl
from jax.experimental.pallas import tpu as pltpu

F32 = jnp.float32
BF16 = jnp.bfloat16

D_MODEL = 2048
BATCH = 16
SEQ = 256
DEPTH = 2
DEC_BATCH = 4
DEC_SEQ = 2048
PAST_LEN = 512
GRID_W = 64
HEAD_DIM = 64
D_ATTN = D_MODEL // 2
NA_HEADS = D_ATTN // HEAD_DIM
NA_KH = 8
NA_KW = 16
D_CONV = D_MODEL // 4
CONV_K = 31
D_SSM = D_MODEL // 4
SSM_P = 64
SSM_HEADS = D_SSM // SSM_P
SSM_GROUPS = 2
SSM_N = 128
SSM_CONV = 5
SSM_CHUNK = 128
D_XBC = D_SSM + 2 * SSM_GROUPS * SSM_N
D_FF = 256 * ((8 * D_MODEL // 3 + 255) // 256)
FFN_CONV = 3
ALPHA = (2 * DEPTH) ** 0.25
LN_EPS = 1e-5
NEG_INF = -1e30

N_CTX = BATCH * SEQ
N_LAT = DEC_BATCH * DEC_SEQ
N_TOK = N_CTX + N_LAT
LANES = 128
SEG = 256
HALO = 16
MOD_ROWS = 8
CTX_MOD_ROW = DEC_BATCH

COL_Q = 0
COL_K = D_ATTN
COL_V = 2 * D_ATTN
COL_XBC = 3 * D_ATTN
COL_U = COL_XBC + D_XBC
COL_G = COL_U + D_CONV
COL_Z = COL_G + D_CONV
COL_DT = COL_Z + D_SSM
D_IN_PAD = COL_DT + LANES

NA_QROWS = 4
NA_QBLK = NA_QROWS * GRID_W
NA_WROWS = NA_QROWS + NA_KH
NA_WIN = NA_WROWS * GRID_W
NA_NQB = DEC_SEQ // NA_QBLK
GRID_ROWS = DEC_SEQ // GRID_W

VMEM_LIMIT = 56 * 1024 * 1024


def _cparams(n_axes):
    return pltpu.CompilerParams(dimension_semantics=("arbitrary",) * n_axes,
                                vmem_limit_bytes=VMEM_LIMIT)


def _sigmoid(x):
    return 1.0 / (1.0 + jnp.exp(-x))


def _silu(x):
    return x * _sigmoid(x)


def _layer_norm(y, g, b):
    mu = jnp.mean(y, axis=-1, keepdims=True)
    yc = y - mu
    var = jnp.mean(yc * yc, axis=-1, keepdims=True)
    return yc * lax.rsqrt(var + LN_EPS) * g + b


def _dot(a, b):
    return jnp.dot(a, b, preferred_element_type=F32)


def _dot_nt(a, b):
    return lax.dot_general(a, b, (((1,), (1,)), ((), ())), preferred_element_type=F32)


def _mod_row_of_tile(i, tm):
    n_ctx_tiles = N_CTX // tm
    return jnp.where(i < n_ctx_tiles, CTX_MOD_ROW, (i - n_ctx_tiles) // (DEC_SEQ // tm))


MOD_TN = 1024


def _mod_kernel(c_ref, w_ref, b_ref, o_ref):
    s = _silu(c_ref[...]).astype(BF16)
    o_ref[...] = _dot(s, w_ref[...].astype(BF16)) + b_ref[...]


def _modulation(cond, w_mod, b_mod):
    nj = (6 * D_MODEL) // MOD_TN
    return pl.pallas_call(
        _mod_kernel,
        grid=(DEPTH, nj),
        in_specs=[
            pl.BlockSpec((MOD_ROWS, D_MODEL), lambda l, j: (0, 0)),
            pl.BlockSpec((None, D_MODEL, MOD_TN), lambda l, j: (l, 0, j)),
            pl.BlockSpec((None, 1, MOD_TN), lambda l, j: (l, 0, j)),
        ],
        out_specs=pl.BlockSpec((None, MOD_ROWS, MOD_TN), lambda l, j: (l, 0, j)),
        out_shape=jax.ShapeDtypeStruct((DEPTH, MOD_ROWS, 6 * D_MODEL), F32),
        compiler_params=_cparams(2),
        name="modulation",
    )(cond, w_mod, b_mod.reshape(DEPTH, 1, 6 * D_MODEL))


IN_TM = 512
IN_TN = 1152


def _inproj_kernel(x_ref, sh_ref, sc_ref, w_ref, o_ref, h_scr):
    @pl.when(pl.program_id(1) == 0)
    def _():
        h_scr[...] = (x_ref[...] * (1.0 + sc_ref[...]) + sh_ref[...]).astype(BF16)

    o_ref[...] = _dot(h_scr[...], w_ref[...])


def _in_projection(x, mod, w_in):
    ni, nj = N_TOK // IN_TM, D_IN_PAD // IN_TN
    mrow = functools.partial(_mod_row_of_tile, tm=IN_TM)
    return pl.pallas_call(
        _inproj_kernel,
        grid=(ni, nj),
        in_specs=[
            pl.BlockSpec((IN_TM, D_MODEL), lambda i, j: (i, 0)),
            pl.BlockSpec((None, None, 1, D_MODEL), lambda i, j: (mrow(i), 0, 0, 0)),
            pl.BlockSpec((None, None, 1, D_MODEL), lambda i, j: (mrow(i), 1, 0, 0)),
            pl.BlockSpec((D_MODEL, IN_TN), lambda i, j: (0, j)),
        ],
        out_specs=pl.BlockSpec((IN_TM, IN_TN), lambda i, j: (i, j)),
        out_shape=jax.ShapeDtypeStruct((N_TOK, D_IN_PAD), F32),
        scratch_shapes=[pltpu.VMEM((IN_TM, D_MODEL), BF16)],
        compiler_params=_cparams(2),
        name="in_projection",
    )(x, mod, mod, w_in)


CONV_RC = 64
SEG_PER_LAT = DEC_SEQ // SEG


def _conv_kernel(u_ref, g_ref, up_ref, gp_ref, un_ref, gn_ref, x_ref, xp_ref, xn_ref, dt_ref,
                 cw_ref, cb_ref, lg_ref, lb_ref, sw_ref, sb_ref, dtb_ref,
                 ob_ref, xa_ref, dts_ref, pad_scr, xpad_scr):
    i = pl.program_id(0)
    n_ctx_seg = N_CTX // SEG
    is_ctx = i < n_ctx_seg
    j = jnp.maximum(i - n_ctx_seg, 0) % SEG_PER_LAT
    first = jnp.logical_or(is_ctx, j == 0)
    last = jnp.logical_or(is_ctx, j == SEG_PER_LAT - 1)

    def glu(u, g):
        return u * _sigmoid(g)

    pad_scr[0:HALO, :] = jnp.where(first, 0.0, glu(up_ref[...], gp_ref[...]))
    pad_scr[HALO:HALO + SEG, :] = glu(u_ref[...], g_ref[...])
    pad_scr[HALO + SEG:, :] = jnp.where(last, 0.0, glu(un_ref[...], gn_ref[...]))
    xpad_scr[0:HALO, :] = jnp.where(first, 0.0, xp_ref[...])
    xpad_scr[HALO:HALO + SEG, :] = x_ref[...]
    xpad_scr[HALO + SEG:, :] = jnp.where(last, 0.0, xn_ref[...])

    off31 = HALO - CONV_K // 2
    off5 = HALO - SSM_CONV // 2
    for rc in range(SEG // CONV_RC):
        r0 = rc * CONV_RC
        acc = jnp.zeros((CONV_RC, D_CONV), F32)
        for k in range(CONV_K):
            acc = acc + cw_ref[k:k + 1, :] * pad_scr[r0 + off31 + k:r0 + off31 + k + CONV_RC, :]
        h = _layer_norm(acc + cb_ref[...], lg_ref[...], lb_ref[...])
        ob_ref[r0:r0 + CONV_RC, :] = _silu(h).astype(BF16)
        for c0 in range(0, D_XBC, D_CONV):
            acc = jnp.zeros((CONV_RC, D_CONV), F32)
            for k in range(SSM_CONV):
                acc = acc + (sw_ref[k:k + 1, c0:c0 + D_CONV]
                             * xpad_scr[r0 + off5 + k:r0 + off5 + k + CONV_RC, c0:c0 + D_CONV])
            xa_ref[r0:r0 + CONV_RC, c0:c0 + D_CONV] = _silu(acc + sb_ref[:, c0:c0 + D_CONV])

    t = dt_ref[...] + dtb_ref[...]
    dts_ref[...] = jnp.maximum(t, 0.0) + jnp.log1p(jnp.exp(-jnp.abs(t)))


def _conv_mixers(p, conv_w, conv_b, ln_g, ln_b, ssm_w, ssm_b, dt_bias):
    nseg = N_TOK // SEG
    hb = SEG // HALO
    n_hb = N_TOK // HALO
    cu, cg = COL_U // D_CONV, COL_G // D_CONV
    cx = COL_XBC // D_XBC
    cdt = COL_DT // LANES

    def main(col, width):
        return pl.BlockSpec((SEG, width), lambda i: (i, col))

    def prev(col, width):
        return pl.BlockSpec((HALO, width), lambda i: (jnp.maximum(i * hb - 1, 0), col))

    def nxt(col, width):
        return pl.BlockSpec((HALO, width), lambda i: (jnp.minimum((i + 1) * hb, n_hb - 1), col))

    def whole(shape):
        return pl.BlockSpec(shape, lambda i: (0,) * len(shape))

    return pl.pallas_call(
        _conv_kernel,
        grid=(nseg,),
        in_specs=[
            main(cu, D_CONV), main(cg, D_CONV), prev(cu, D_CONV), prev(cg, D_CONV),
            nxt(cu, D_CONV), nxt(cg, D_CONV),
            main(cx, D_XBC), prev(cx, D_XBC), nxt(cx, D_XBC),
            main(cdt, LANES),
            whole((CONV_K, D_CONV)), whole((1, D_CONV)), whole((1, D_CONV)), whole((1, D_CONV)),
            whole((SSM_CONV, D_XBC)), whole((1, D_XBC)), whole((1, LANES)),
        ],
        out_specs=[
            pl.BlockSpec((SEG, D_CONV), lambda i: (i, 0)),
            pl.BlockSpec((SEG, D_XBC), lambda i: (i, 0)),
            pl.BlockSpec((SEG, LANES), lambda i: (i, 0)),
        ],
        out_shape=[
            jax.ShapeDtypeStruct((N_TOK, D_CONV), BF16),
            jax.ShapeDtypeStruct((N_TOK, D_XBC), F32),
            jax.ShapeDtypeStruct((N_TOK, LANES), F32),
        ],
        scratch_shapes=[pltpu.VMEM((SEG + 2 * HALO, D_CONV), F32),
                        pltpu.VMEM((SEG + 2 * HALO, D_XBC), F32)],
        compiler_params=_cparams(1),
        name="conv_mixers",
    )(p, p, p, p, p, p, p, p, p, p,
      conv_w, conv_b.reshape(1, D_CONV), ln_g.reshape(1, D_CONV), ln_b.reshape(1, D_CONV),
      ssm_w, ssm_b.reshape(1, D_XBC), dt_bias)


GROUP_W = (SSM_HEADS // SSM_GROUPS) * SSM_P


def _ssd_kernel(*refs, seq_len, has_h0, emit_h):
    xa_ref, dt_ref, z_ref, alog_ref, d_ref, ng_ref = refs[:6]
    pos = 6
    h0_ref = None
    if has_h0:
        h0_ref = refs[pos]
        pos += 1
    oc_ref = refs[pos]
    pos += 1
    hf_ref = None
    if emit_h:
        hf_ref = refs[pos]
        pos += 1
    y_scr, h_scr = refs[pos], refs[pos + 1]

    q = SSM_CHUNK
    nc = seq_len // q
    a_row = -jnp.exp(alog_ref[...])
    if has_h0:
        h_scr[...] = h0_ref[...]
    else:
        h_scr[...] = jnp.zeros_like(h_scr)
    y_scr[...] = d_ref[...] * xa_ref[:, 0:D_SSM]

    row = lax.broadcasted_iota(jnp.int32, (q, q), 0)
    col = lax.broadcasted_iota(jnp.int32, (q, q), 1)
    tril = row >= col
    triu = col >= row
    ones_tril = jnp.where(tril, 1.0, 0.0).astype(BF16)
    lo_half = lax.broadcasted_iota(jnp.int32, (1, LANES), 1) < SSM_P

    def chunk_step(c, carry):
        for d in range(2):
            cc = c if d == 0 else nc - 1 - c
            r0 = pl.multiple_of(cc * q, q)
            dt = dt_ref[pl.ds(r0, q), :]
            dta = dt * a_row
            p_hi = dta.astype(BF16)
            rem = dta - p_hi.astype(F32)
            p_mid = rem.astype(BF16)
            p_lo = (rem - p_mid.astype(F32)).astype(BF16)
            pre = _dot(ones_tril, p_hi) + _dot(ones_tril, p_mid) + _dot(ones_tril, p_lo)
            tot = pre[q - 1:q, :]
            cs = pre if d == 0 else (tot - pre + dta)
            cs_t = cs.T
            mask = tril if d == 0 else triu
            for g in range(SSM_GROUPS):
                bg = xa_ref[pl.ds(r0, q), D_SSM + g * SSM_N:D_SSM + (g + 1) * SSM_N].astype(BF16)
                cg = xa_ref[pl.ds(r0, q), D_SSM + (SSM_GROUPS + g) * SSM_N:
                            D_SSM + (SSM_GROUPS + g + 1) * SSM_N].astype(BF16)
                gmat = _dot_nt(cg, bg)
                hg = h_scr[d, g * GROUP_W:(g + 1) * GROUP_W, :]
                y_off = _dot_nt(cg, hg.astype(BF16))
                xds = []
                scales = []
                for hp in range(2):
                    h_a = g * 4 + hp * 2
                    lane0 = h_a * SSM_P
                    cols = (d * SSM_HEADS + h_a, d * SSM_HEADS + h_a + 1)
                    cs_c = [cs[:, cidx:cidx + 1] for cidx in cols]
                    dt_c = [dt[:, cidx:cidx + 1] for cidx in cols]
                    tot_c = [tot[:, cidx:cidx + 1] for cidx in cols]
                    xs2 = xa_ref[pl.ds(r0, q), lane0:lane0 + LANES]
                    xdt2 = xs2 * jnp.where(lo_half, dt_c[0], dt_c[1])
                    y2 = y_off[:, hp * LANES:(hp + 1) * LANES] * jnp.where(
                        lo_half, jnp.exp(cs_c[0]), jnp.exp(cs_c[1]))
                    for e in range(2):
                        seg = jnp.exp(jnp.where(mask, cs_c[e] - cs_t[cols[e]:cols[e] + 1, :], NEG_INF))
                        sc = (gmat * seg).astype(BF16)
                        half = lo_half if e == 0 else jnp.logical_not(lo_half)
                        y2 = y2 + _dot(sc, jnp.where(half, xdt2, 0.0).astype(BF16))
                        scales.append(jnp.broadcast_to(jnp.exp(tot_c[e]), (SSM_P, SSM_N)))
                    y_scr[pl.ds(r0, q), lane0:lane0 + LANES] += y2
                    decay2 = jnp.where(lo_half, jnp.exp(tot_c[0] - cs_c[0]), jnp.exp(tot_c[1] - cs_c[1]))
                    xds.append(xdt2 * decay2)
                x4 = jnp.concatenate(xds, axis=1)
                s4 = _dot(x4.T.astype(BF16), bg)
                h_scr[d, g * GROUP_W:(g + 1) * GROUP_W, :] = hg * jnp.concatenate(scales, axis=0) + s4
        return carry

    lax.fori_loop(0, nc, chunk_step, 0)

    def finish(r, carry):
        r0 = pl.multiple_of(r * q, q)
        y = y_scr[pl.ds(r0, q), :] * _silu(z_ref[pl.ds(r0, q), :])
        y = y * lax.rsqrt(jnp.mean(y * y, axis=-1, keepdims=True) + LN_EPS) * ng_ref[...]
        oc_ref[pl.ds(r0, q), :] = y.astype(BF16)
        return carry

    lax.fori_loop(0, nc, finish, 0)
    if emit_h:
        hf_ref[...] = h_scr[...]


def _ssd_scan(xa, dts, p, a_log, d_lanes, norm_g, h0, *, n_seq, seq_len, row0, emit_h):
    blk0 = row0 // seq_len
    zc = COL_Z // D_SSM
    has_h0 = h0 is not None
    in_specs = [
        pl.BlockSpec((seq_len, D_XBC), lambda b: (blk0 + b, 0)),
        pl.BlockSpec((seq_len, LANES), lambda b: (blk0 + b, 0)),
        pl.BlockSpec((seq_len, D_SSM), lambda b: (blk0 + b, zc)),
        pl.BlockSpec((1, LANES), lambda b: (0, 0)),
        pl.BlockSpec((1, D_SSM), lambda b: (0, 0)),
        pl.BlockSpec((1, D_SSM), lambda b: (0, 0)),
    ]
    args = [xa, dts, p, a_log, d_lanes, norm_g]
    if has_h0:
        in_specs.append(pl.BlockSpec((None, 2, D_SSM, SSM_N), lambda b: (b, 0, 0, 0)))
        args.append(h0)
    out_specs = [pl.BlockSpec((seq_len, D_SSM), lambda b: (b, 0))]
    out_shape = [jax.ShapeDtypeStruct((n_seq * seq_len, D_SSM), BF16)]
    if emit_h:
        out_specs.append(pl.BlockSpec((None, 2, D_SSM, SSM_N), lambda b: (b, 0, 0, 0)))
        out_shape.append(jax.ShapeDtypeStruct((n_seq, 2, D_SSM, SSM_N), F32))
    return pl.pallas_call(
        functools.partial(_ssd_kernel, seq_len=seq_len, has_h0=has_h0, emit_h=emit_h),
        grid=(n_seq,),
        in_specs=in_specs,
        out_specs=out_specs,
        out_shape=out_shape,
        scratch_shapes=[pltpu.VMEM((seq_len, D_SSM), F32), pltpu.VMEM((2, D_SSM, SSM_N), F32)],
        compiler_params=_cparams(1),
        name="ssd_scan_%d" % seq_len,
    )(*args)


def _head_masks():
    lane = lax.broadcasted_iota(jnp.int32, (1, LANES), 1)
    return lane < HEAD_DIM, lane >= HEAD_DIM


def _ctx_attn_kernel(q_ref, k_ref, v_ref, o_ref):
    q = q_ref[...] * (HEAD_DIM ** -0.5)
    k = k_ref[...].astype(BF16)
    v = v_ref[...]
    outs = []
    for m in _head_masks():
        s = _dot_nt(jnp.where(m, q, 0.0).astype(BF16), k)
        e = jnp.exp(s - jnp.max(s, axis=-1, keepdims=True))
        acc = _dot(e.astype(BF16), jnp.where(m, v, 0.0).astype(BF16))
        outs.append(acc / jnp.sum(e, axis=-1, keepdims=True))
    o_ref[...] = jnp.where(_head_masks()[0], outs[0], outs[1]).astype(BF16)


def _context_attention(p):
    nhp = D_ATTN // LANES
    return pl.pallas_call(
        _ctx_attn_kernel,
        grid=(BATCH, nhp),
        in_specs=[
            pl.BlockSpec((SEQ, LANES), lambda b, h: (b, COL_Q // LANES + h)),
            pl.BlockSpec((SEQ, LANES), lambda b, h: (b, COL_K // LANES + h)),
            pl.BlockSpec((SEQ, LANES), lambda b, h: (b, COL_V // LANES + h)),
        ],
        out_specs=pl.BlockSpec((SEQ, LANES), lambda b, h: (b, h)),
        out_shape=jax.ShapeDtypeStruct((N_CTX, D_ATTN), BF16),
        compiler_params=_cparams(2),
        name="context_attention",
    )(p, p, p)


def _na_window_start(qi):
    return jnp.clip(NA_QROWS * qi - NA_KH // 2, 0, GRID_ROWS - NA_WROWS)


def _na_bias_block(pattern, a, j):
    if pattern == 0:
        lo, rel = 0, j - a + NA_KH - 1
    elif pattern == 1:
        lo, rel = a, j - a + NA_KH // 2 - 1
    else:
        lo, rel = NA_WROWS - NA_KH, j - a - 1
    return rel if lo <= j < lo + NA_KH else None


def _na_bias_kernel(rpb_ref, o_ref):
    c = lax.broadcasted_iota(jnp.int32, (GRID_W, LANES), 0)
    lane = lax.broadcasted_iota(jnp.int32, (GRID_W, LANES), 1)
    kc = lane & (GRID_W - 1)
    start = jnp.clip(c - NA_KW // 2, 0, GRID_W - NA_KW)
    in_win = jnp.logical_and(kc >= start, kc < start + NA_KW)
    rel_c = kc - c + NA_KW - 1
    neg = jnp.full((GRID_W, LANES), NEG_INF, F32)
    n_dr, n_dc = 2 * NA_KH - 1, 2 * NA_KW - 1
    base = (pl.program_id(0) * NA_HEADS + pl.program_id(1)) * (n_dr * n_dc)
    rows = []
    for dr in range(n_dr):
        w = neg
        for dc in range(n_dc):
            w = jnp.where(rel_c == dc, rpb_ref[base + dr * n_dc + dc], w)
        rows.append(jnp.where(in_win, w, NEG_INF))
    lo_half = lane < GRID_W
    for pattern in range(3):
        for a in range(NA_QROWS):
            for jp in range(NA_WROWS // 2):
                r_a = _na_bias_block(pattern, a, 2 * jp)
                r_b = _na_bias_block(pattern, a, 2 * jp + 1)
                blk_a = neg if r_a is None else rows[r_a]
                blk_b = neg if r_b is None else rows[r_b]
                o_ref[pattern, a * GRID_W:(a + 1) * GRID_W, jp * LANES:(jp + 1) * LANES] = jnp.where(
                    lo_half, blk_a, blk_b)


def _na_bias(rpb):
    return pl.pallas_call(
        _na_bias_kernel,
        grid=(DEPTH, NA_HEADS),
        in_specs=[pl.BlockSpec(memory_space=pltpu.SMEM)],
        out_specs=pl.BlockSpec((None, 3, None, NA_QBLK, NA_WIN), lambda l, h: (l, 0, h, 0, 0)),
        out_shape=jax.ShapeDtypeStruct((DEPTH, 3, NA_HEADS, NA_QBLK, NA_WIN), F32),
        compiler_params=_cparams(2),
        name="na_bias",
    )(rpb.reshape(-1))


def _na_kernel(q_ref, k_ref, v_ref, kc_ref, vc_ref, bias_ref, o_ref):
    qi = pl.program_id(2)
    r0 = pl.multiple_of(_na_window_start(qi) * GRID_W, GRID_W)
    pattern = jnp.where(qi == 0, 0, jnp.where(qi == NA_NQB - 1, 2, 1))
    q = q_ref[...] * (HEAD_DIM ** -0.5)
    kw = k_ref[pl.ds(r0, NA_WIN), :].astype(BF16)
    vw = v_ref[pl.ds(r0, NA_WIN), :]
    kc = kc_ref[...].astype(BF16)
    vc = vc_ref[...]
    outs = []
    for h, m in enumerate(_head_masks()):
        qh = jnp.where(m, q, 0.0).astype(BF16)
        s_lat = _dot_nt(qh, kw) + bias_ref[pattern, h]
        s_ctx = _dot_nt(qh, kc)
        mx = jnp.maximum(jnp.max(s_lat, axis=-1, keepdims=True), jnp.max(s_ctx, axis=-1, keepdims=True))
        e_lat = jnp.exp(s_lat - mx)
        e_ctx = jnp.exp(s_ctx - mx)
        den = jnp.sum(e_lat, axis=-1, keepdims=True) + jnp.sum(e_ctx, axis=-1, keepdims=True)
        acc = (_dot(e_lat.astype(BF16), jnp.where(m, vw, 0.0).astype(BF16))
               + _dot(e_ctx.astype(BF16), jnp.where(m, vc, 0.0).astype(BF16)))
        outs.append(acc / den)
    o_ref[...] = jnp.where(_head_masks()[0], outs[0], outs[1]).astype(BF16)


def _neighborhood_attention(p, cache_k, cache_v, bias, layer):
    nhp = D_ATTN // LANES
    qblk0 = N_CTX // NA_QBLK
    seq0 = N_CTX // DEC_SEQ
    return pl.pallas_call(
        _na_kernel,
        grid=(nhp, DEC_BATCH, NA_NQB),
        in_specs=[
            pl.BlockSpec((NA_QBLK, LANES), lambda h, b, i: (qblk0 + b * NA_NQB + i, COL_Q // LANES + h)),
            pl.BlockSpec((DEC_SEQ, LANES), lambda h, b, i: (seq0 + b, COL_K // LANES + h)),
            pl.BlockSpec((DEC_SEQ, LANES), lambda h, b, i: (seq0 + b, COL_V // LANES + h)),
            pl.BlockSpec((None, None, PAST_LEN, LANES), lambda h, b, i: (b, layer, 0, h)),
            pl.BlockSpec((None, None, PAST_LEN, LANES), lambda h, b, i: (b, layer, 0, h)),
            pl.BlockSpec((None, 3, 2, NA_QBLK, NA_WIN), lambda h, b, i: (layer, 0, h, 0, 0)),
        ],
        out_specs=pl.BlockSpec((NA_QBLK, LANES), lambda h, b, i: (b * NA_NQB + i, h)),
        out_shape=jax.ShapeDtypeStruct((N_LAT, D_ATTN), BF16),
        compiler_params=_cparams(3),
        name="neighborhood_attention",
    )(p, p, p, cache_k, cache_v, bias)


OUT_TM = 512


def _outproj_kernel(x_ref, ac_ref, al_ref, ob_ref, cc_ref, cl_ref, wa_ref, wb_ref, wc_ref,
                    g1_ref, sh2_ref, sc2_ref, lg_ref, lb_ref, x1_ref, h2_ref):
    is_ctx = pl.program_id(0) < N_CTX // OUT_TM
    oa = jnp.where(is_ctx, ac_ref[...], al_ref[...])
    oc = jnp.where(is_ctx, cc_ref[...], cl_ref[...])
    mix = _dot(oa, wa_ref[...]) + _dot(ob_ref[...], wb_ref[...]) + _dot(oc, wc_ref[...])
    x1 = _layer_norm(ALPHA * x_ref[...] + g1_ref[...] * mix, lg_ref[...], lb_ref[...])
    x1_ref[...] = x1
    h2_ref[...] = (x1 * (1.0 + sc2_ref[...]) + sh2_ref[...]).astype(BF16)


def _out_projection(x, oa_ctx, oa_lat, ob, oc_ctx, oc_lat, w_out, mod, ln_g, ln_b):
    ni = N_TOK // OUT_TM
    nct = N_CTX // OUT_TM
    mrow = functools.partial(_mod_row_of_tile, tm=OUT_TM)

    def ctx_rows(width):
        return pl.BlockSpec((OUT_TM, width), lambda i: (jnp.minimum(i, nct - 1), 0))

    def lat_rows(width):
        return pl.BlockSpec((OUT_TM, width), lambda i: (jnp.maximum(i - nct, 0), 0))

    def modspec(k):
        return pl.BlockSpec((None, None, 1, D_MODEL), lambda i: (mrow(i), k, 0, 0))

    vec = pl.BlockSpec((1, D_MODEL), lambda i: (0, 0))
    return pl.pallas_call(
        _outproj_kernel,
        grid=(ni,),
        in_specs=[
            pl.BlockSpec((OUT_TM, D_MODEL), lambda i: (i, 0)),
            ctx_rows(D_ATTN), lat_rows(D_ATTN),
            pl.BlockSpec((OUT_TM, D_CONV), lambda i: (i, 0)),
            ctx_rows(D_SSM), lat_rows(D_SSM),
            pl.BlockSpec((D_ATTN, D_MODEL), lambda i: (0, 0)),
            pl.BlockSpec((D_CONV, D_MODEL), lambda i: (D_ATTN // D_CONV, 0)),
            pl.BlockSpec((D_SSM, D_MODEL), lambda i: ((D_ATTN + D_CONV) // D_SSM, 0)),
            modspec(2), modspec(3), modspec(4), vec, vec,
        ],
        out_specs=[pl.BlockSpec((OUT_TM, D_MODEL), lambda i: (i, 0)),
                   pl.BlockSpec((OUT_TM, D_MODEL), lambda i: (i, 0))],
        out_shape=[jax.ShapeDtypeStruct((N_TOK, D_MODEL), F32),
                   jax.ShapeDtypeStruct((N_TOK, D_MODEL), BF16)],
        compiler_params=_cparams(1),
        name="out_projection",
    )(x, oa_ctx, oa_lat, ob, oc_ctx, oc_lat, w_out, w_out, w_out, mod, mod, mod,
      ln_g.reshape(1, D_MODEL), ln_b.reshape(1, D_MODEL))


FFN_TM = 512
FFN_TF = 512


def _ffn_kernel(h_ref, hp_ref, hn_ref, wa_ref, wg_ref, cwa_ref, cwg_ref, cba_ref, cbg_ref, wd_ref,
                x1_ref, g2_ref, lg_ref, lb_ref, o_ref, hext_scr, acc_scr):
    i = pl.program_id(0)
    j = pl.program_id(1)
    ext = FFN_TM + 2 * HALO

    @pl.when(j == 0)
    def _():
        hext_scr[0:HALO, :] = hp_ref[...]
        hext_scr[HALO:HALO + FFN_TM, :] = h_ref[...]
        hext_scr[HALO + FFN_TM:, :] = hn_ref[...]
        acc_scr[...] = jnp.zeros_like(acc_scr)

    seq_len = jnp.where(i < N_CTX // FFN_TM, SEQ, DEC_SEQ)
    pos = (i * FFN_TM + lax.broadcasted_iota(jnp.int32, (FFN_TM, 1), 0)) & (seq_len - 1)
    has_prev = pos != 0
    has_next = pos != seq_len - 1

    def conv(w_ref, cw_ref, cb_ref):
        u = _dot(hext_scr[...], w_ref[...])
        u_prev = pltpu.roll(u, 1, axis=0)[HALO:HALO + FFN_TM, :]
        u_next = pltpu.roll(u, ext - 1, axis=0)[HALO:HALO + FFN_TM, :]
        return (cw_ref[0:1, :] * jnp.where(has_prev, u_prev, 0.0)
                + cw_ref[1:2, :] * u[HALO:HALO + FFN_TM, :]
                + cw_ref[2:3, :] * jnp.where(has_next, u_next, 0.0)
                + cb_ref[...])

    act = conv(wa_ref, cwa_ref, cba_ref) * _silu(conv(wg_ref, cwg_ref, cbg_ref))
    acc_scr[...] += _dot(act.astype(BF16), wd_ref[...])

    @pl.when(j == pl.num_programs(1) - 1)
    def _():
        o_ref[...] = _layer_norm(ALPHA * x1_ref[...] + g2_ref[...] * acc_scr[...], lg_ref[...], lb_ref[...])


def _conv_ffn(x1, h2, w_up, conv_w, conv_b, w_down, mod, ln_g, ln_b):
    ni, nj = N_TOK // FFN_TM, D_FF // FFN_TF
    hb = FFN_TM // HALO
    n_hb = N_TOK // HALO
    mrow = functools.partial(_mod_row_of_tile, tm=FFN_TM)
    vec = pl.BlockSpec((1, D_MODEL), lambda i, j: (0, 0))
    return pl.pallas_call(
        _ffn_kernel,
        grid=(ni, nj),
        in_specs=[
            pl.BlockSpec((FFN_TM, D_MODEL), lambda i, j: (i, 0)),
            pl.BlockSpec((HALO, D_MODEL), lambda i, j: (jnp.maximum(i * hb - 1, 0), 0)),
            pl.BlockSpec((HALO, D_MODEL), lambda i, j: (jnp.minimum((i + 1) * hb, n_hb - 1), 0)),
            pl.BlockSpec((D_MODEL, FFN_TF), lambda i, j: (0, j)),
            pl.BlockSpec((D_MODEL, FFN_TF), lambda i, j: (0, nj + j)),
            pl.BlockSpec((FFN_CONV, FFN_TF), lambda i, j: (0, j)),
            pl.BlockSpec((FFN_CONV, FFN_TF), lambda i, j: (0, nj + j)),
            pl.BlockSpec((1, FFN_TF), lambda i, j: (0, j)),
            pl.BlockSpec((1, FFN_TF), lambda i, j: (0, nj + j)),
            pl.BlockSpec((FFN_TF, D_MODEL), lambda i, j: (j, 0)),
            pl.BlockSpec((FFN_TM, D_MODEL), lambda i, j: (i, 0)),
            pl.BlockSpec((None, None, 1, D_MODEL), lambda i, j: (mrow(i), 5, 0, 0)),
            vec, vec,
        ],
        out_specs=pl.BlockSpec((FFN_TM, D_MODEL), lambda i, j: (i, 0)),
        out_shape=jax.ShapeDtypeStruct((N_TOK, D_MODEL), F32),
        scratch_shapes=[pltpu.VMEM((FFN_TM + 2 * HALO, D_MODEL), BF16),
                        pltpu.VMEM((FFN_TM, D_MODEL), F32)],
        compiler_params=_cparams(2),
        name="conv_ffn",
    )(h2, h2, h2, w_up, w_up, conv_w, conv_w, conv_b, conv_b, w_down, x1, mod,
      ln_g.reshape(1, D_MODEL), ln_b.reshape(1, D_MODEL))


def _permute_in_columns(w_in):
    n_qkv = 3 * D_ATTN
    n_ugz = 2 * D_CONV + D_SSM
    pad = jnp.zeros((D_MODEL, D_IN_PAD - (n_qkv + n_ugz + D_XBC + 2 * SSM_HEADS)), w_in.dtype)
    return jnp.concatenate([
        w_in[:, :n_qkv],
        w_in[:, n_qkv + n_ugz:n_qkv + n_ugz + D_XBC],
        w_in[:, n_qkv:n_qkv + n_ugz],
        w_in[:, n_qkv + n_ugz + D_XBC:],
        pad], axis=1)


def _pad_lanes(v):
    flat = v.reshape(1, 2 * SSM_HEADS).astype(F32)
    return jnp.pad(flat, ((0, 0), (0, LANES - 2 * SSM_HEADS)))


def kernel(x_prompt, x_sample, cache_k, cache_v, state_ssm, c, c_ctx, w_mod, b_mod, w_in, rpb,
           conv_w, conv_b, conv_ln_g, conv_ln_b, ssm_conv_w, ssm_conv_b, ssm_a_log, ssm_dt_bias,
           ssm_d, ssm_norm_g, w_out, ln1_g, ln1_b, w_up, ffn_conv_w, ffn_conv_b, w_down, ln2_g, ln2_b):
    x = jnp.concatenate([x_prompt.reshape(N_CTX, D_MODEL), x_sample.reshape(N_LAT, D_MODEL)], axis=0)
    cond = jnp.concatenate([c, c_ctx[None, :], jnp.zeros((MOD_ROWS - DEC_BATCH - 1, D_MODEL), F32)], axis=0)
    mod_all = _modulation(cond, w_mod, b_mod).reshape(DEPTH, MOD_ROWS, 6, 1, D_MODEL)
    bias_all = _na_bias(rpb)
    ck = cache_k.reshape(DEC_BATCH, DEPTH, PAST_LEN, D_ATTN)
    cv = cache_v.reshape(DEC_BATCH, DEPTH, PAST_LEN, D_ATTN)
    h0_all = state_ssm.reshape(DEC_BATCH, DEPTH, 2, D_SSM, SSM_N)

    ks, vs, hs = [], [], []
    for l in range(DEPTH):
        mod = mod_all[l]
        p = _in_projection(x, mod, _permute_in_columns(w_in[l]).astype(BF16))
        ob, xa, dts = _conv_mixers(p, conv_w[l], conv_b[l], conv_ln_g[l], conv_ln_b[l],
                                   ssm_conv_w[l], ssm_conv_b[l], _pad_lanes(ssm_dt_bias[l]))
        a_log = _pad_lanes(ssm_a_log[l])
        d_lanes = jnp.repeat(ssm_d[l].astype(F32), SSM_P).reshape(1, D_SSM)
        norm_g = ssm_norm_g[l].reshape(1, D_SSM)
        oc_ctx, h_fin = _ssd_scan(xa, dts, p, a_log, d_lanes, norm_g, None,
                                  n_seq=BATCH, seq_len=SEQ, row0=0, emit_h=True)
        (oc_lat,) = _ssd_scan(xa, dts, p, a_log, d_lanes, norm_g, h0_all[:, l],
                              n_seq=DEC_BATCH, seq_len=DEC_SEQ, row0=N_CTX, emit_h=False)
        oa_ctx = _context_attention(p)
        oa_lat = _neighborhood_attention(p, ck, cv, bias_all, l)
        x1, h2 = _out_projection(x, oa_ctx, oa_lat, ob, oc_ctx, oc_lat, w_out[l].astype(BF16), mod,
                                 ln1_g[l], ln1_b[l])
        x = _conv_ffn(x1, h2, w_up[l].astype(BF16), ffn_conv_w[l], ffn_conv_b[l].reshape(1, 2 * D_FF),
                      w_down[l].astype(BF16), mod, ln2_g[l], ln2_b[l])
        ks.append(p[:N_CTX, COL_K:COL_K + D_ATTN].reshape(BATCH, SEQ, NA_HEADS, HEAD_DIM))
        vs.append(p[:N_CTX, COL_V:COL_V + D_ATTN].reshape(BATCH, SEQ, NA_HEADS, HEAD_DIM))
        hs.append(h_fin.reshape(BATCH, 2, SSM_HEADS, SSM_P, SSM_N))

    y_prompt = x[:N_CTX].reshape(BATCH, SEQ, D_MODEL)
    y_sample = x[N_CTX:].reshape(DEC_BATCH, DEC_SEQ, D_MODEL)
    return (y_prompt, y_sample, jnp.stack(ks, axis=1), jnp.stack(vs, axis=1), jnp.stack(hs, axis=1))
```
